```python
import math
import jax, jax.numpy as jnp
from jax import lax
import numpy as np

D_MODEL = 1024
BATCH = 2
SEQ = 16384
DEPTH = 2

N_META = 16
N_MIXERS = 2
N_A_LAYERS = (DEPTH + 1) // 2
N_B_LAYERS = DEPTH // 2
SC_WIDTH = 3
D_RNN = 1280
RG_BLOCKS = 10
RG_BLOCK_DIM = D_RNN // RG_BLOCKS
RG_CONV_WIDTH = 4
RG_C = 8.0
D_FF = 2816
FFN_CONV_WIDTH = 3
RMS_EPS = 1e-6

kernel_name = "hybrid_shortconv_rglru_convffn"


def rms_norm(x, g):
    xf = x.astype(jnp.float32)
    var = jnp.mean(xf * xf, axis=-1, keepdims=True)
    return (xf * lax.rsqrt(var + RMS_EPS) * g.astype(jnp.float32)).astype(x.dtype)


def causal_dwconv(x, w):
    k_width = w.shape[0]
    t_len = x.shape[1]
    xp = jnp.pad(x, ((0, 0), (k_width - 1, 0), (0, 0)))
    y = xp[:, 0:t_len] * w[0]
    for k in range(1, k_width):
        y = y + xp[:, k:k + t_len] * w[k]
    return y


def short_conv_mixer(x, w_in, conv_w, w_out):
    h = jnp.einsum('btd,de->bte', x, w_in)
    b_gate, c_gate, v = jnp.split(h, 3, axis=-1)
    u = causal_dwconv(c_gate * v, conv_w)
    return jnp.einsum('btd,de->bte', b_gate * u, w_out)


def _lin_rec_combine(left, right):
    a_l, b_l = left
    a_r, b_r = right
    return a_l * a_r, a_r * b_l + b_r


def rglru_block(x, w_in, conv_w, conv_b, w_gate_a, b_gate_a, w_gate_x, b_gate_x, lam, w_out):
    bsz, t_len, _ = x.shape
    h = jnp.einsum('btd,de->bte', x, w_in)
    g_branch, r_branch = jnp.split(h, 2, axis=-1)
    gate = jax.nn.gelu(g_branch, approximate=True)
    u = causal_dwconv(r_branch, conv_w) + conv_b
    ub = u.reshape(bsz, t_len, RG_BLOCKS, RG_BLOCK_DIM)
    r = jax.nn.sigmoid(jnp.einsum('btki,kij->btkj', ub, w_gate_a).reshape(bsz, t_len, D_RNN) + b_gate_a)
    i = jax.nn.sigmoid(jnp.einsum('btki,kij->btkj', ub, w_gate_x).reshape(bsz, t_len, D_RNN) + b_gate_x)
    log_a = -RG_C * r.astype(jnp.float32) * jax.nn.softplus(-lam.astype(jnp.float32))
    a = jnp.exp(log_a)
    mult = jnp.sqrt(-jnp.expm1(2.0 * log_a))
    b = mult * (i * u).astype(jnp.float32)
    _, hs = lax.associative_scan(_lin_rec_combine, (a, b), axis=1)
    y = hs.astype(x.dtype) * gate
    return jnp.einsum('bte,ed->btd', y, w_out)


def conv_gated_mlp(x, w_up, conv_w, w_down):
    h = jnp.einsum('btd,df->btf', x, w_up)
    h = causal_dwconv(h, conv_w)
    g, v = jnp.split(h, 2, axis=-1)
    return jnp.einsum('btf,fd->btd', jax.nn.silu(g) * v, w_down)


def setup_inputs(seed: int = 0) -> dict:
    key = jax.random.key(seed)
    ks = jax.random.split(key, 24)
    f32 = jnp.float32
    D = D_MODEL

    def nrm(k, shape, scale):
        return jax.random.normal(k, shape, f32) * scale

    x = jax.random.normal(ks[0], (BATCH, SEQ, D), f32)
    meta_tokens = nrm(ks[1], (N_META, D), 1.0)
    norm_mix_g = 1.0 + nrm(ks[2], (DEPTH, D), 0.01)
    norm_ffn_g = 1.0 + nrm(ks[3], (DEPTH, D), 0.01)
    final_norm_g = 1.0 + nrm(ks[4], (D,), 0.01)

    sc_w_in = nrm(ks[5], (N_A_LAYERS, D, 3 * D), D ** -0.5)
    sc_conv_w = nrm(ks[6], (N_A_LAYERS, SC_WIDTH, D), SC_WIDTH ** -0.5)
    sc_w_out = nrm(ks[7], (N_A_LAYERS, D, D), D ** -0.5)

    rg_w_in = nrm(ks[8], (N_B_LAYERS, D, 2 * D_RNN), D ** -0.5)
    rg_conv_w = nrm(ks[9], (N_B_LAYERS, RG_CONV_WIDTH, D_RNN), RG_CONV_WIDTH ** -0.5)
    rg_conv_b = nrm(ks[10], (N_B_LAYERS, D_RNN), 0.01)
    rg_w_gate_a = nrm(ks[11], (N_B_LAYERS, RG_BLOCKS, RG_BLOCK_DIM, RG_BLOCK_DIM), RG_BLOCK_DIM ** -0.5)
    rg_b_gate_a = nrm(ks[12], (N_B_LAYERS, D_RNN), 0.01)
    rg_w_gate_x = nrm(ks[13], (N_B_LAYERS, RG_BLOCKS, RG_BLOCK_DIM, RG_BLOCK_DIM), RG_BLOCK_DIM ** -0.5)
    rg_b_gate_x = nrm(ks[14], (N_B_LAYERS, D_RNN), 0.01)
    a_c = jax.random.uniform(ks[15], (N_B_LAYERS, D_RNN), f32, 0.9, 0.999)
    a_base = a_c ** (1.0 / RG_C)
    rg_lambda = jnp.log(a_base) - jnp.log1p(-a_base)
    rg_w_out = nrm(ks[16], (N_B_LAYERS, D_RNN, D), D_RNN ** -0.5)

    ffn_w_up = nrm(ks[17], (DEPTH, D, 2 * D_FF), D ** -0.5)
    ffn_conv_w = nrm(ks[18], (DEPTH, FFN_CONV_WIDTH, 2 * D_FF), FFN_CONV_WIDTH ** -0.5)
    ffn_w_down = nrm(ks[19], (DEPTH, D_FF, D), D_FF ** -0.5)

    return {"x": x, "meta_tokens": meta_tokens, "norm_mix_g": norm_mix_g,
            "norm_ffn_g": norm_ffn_g, "final_norm_g": final_norm_g,
            "sc_w_in": sc_w_in, "sc_conv_w": sc_conv_w, "sc_w_out": sc_w_out,
            "rg_w_in": rg_w_in, "rg_conv_w": rg_conv_w, "rg_conv_b": rg_conv_b,
            "rg_w_gate_a": rg_w_gate_a, "rg_b_gate_a": rg_b_gate_a,
            "rg_w_gate_x": rg_w_gate_x, "rg_b_gate_x": rg_b_gate_x,
            "rg_lambda": rg_lambda, "rg_w_out": rg_w_out,
            "ffn_w_up": ffn_w_up, "ffn_conv_w": ffn_conv_w, "ffn_w_down": ffn_w_down}


def reference(x, meta_tokens, norm_mix_g, norm_ffn_g, final_norm_g,
              sc_w_in, sc_conv_w, sc_w_out,
              rg_w_in, rg_conv_w, rg_conv_b, rg_w_gate_a, rg_b_gate_a,
              rg_w_gate_x, rg_b_gate_x, rg_lambda, rg_w_out,
              ffn_w_up, ffn_conv_w, ffn_w_down):
    bsz = x.shape[0]
    meta = jnp.broadcast_to(meta_tokens.astype(x.dtype)[None], (bsz, N_META, x.shape[-1]))
    h = jnp.concatenate([meta, x], axis=1)
    for layer in range(DEPTH):
        hn = rms_norm(h, norm_mix_g[layer])
        j = layer // N_MIXERS
        if layer % N_MIXERS == 0:
            mix = short_conv_mixer(hn, sc_w_in[j], sc_conv_w[j], sc_w_out[j])
        else:
            mix = rglru_block(hn, rg_w_in[j], rg_conv_w[j], rg_conv_b[j],
                              rg_w_gate_a[j], rg_b_gate_a[j], rg_w_gate_x[j], rg_b_gate_x[j],
                              rg_lambda[j], rg_w_out[j])
        h = h + mix
        h = h + conv_gated_mlp(rms_norm(h, norm_ffn_g[layer]), ffn_w_up[layer],
                               ffn_conv_w[layer], ffn_w_down[layer])
    out = rms_norm(h, final_norm_g)
    return out[:, N_META:]
```

```python
import functools

import jax
import jax.numpy as jnp
from jax import lax
from jax.experimental import pallas as pl
from jax.experimental.pallas import tpu as pltpu

F32 = jnp.float32
BF16 = jnp.bfloat16

RMS_EPS = 1e-6
RG_C = 8.0
SUBLANES = 8
LANES = 128
CARRY_ROWS = SUBLANES
VMEM_LIMIT_BYTES = 56 * 1024 * 1024
TIME_TILE = 512


def _rms_scale(x, g):
    var = jnp.mean(x * x, axis=-1, keepdims=True)
    return x * lax.rsqrt(var + RMS_EPS) * g


def _sigmoid(x):
    return 1.0 / (1.0 + jnp.exp(-x))


def _causal_conv(buf_ref, w_ref, tt, cols, width):
    acc = None
    for k in range(width):
        start = CARRY_ROWS - (width - 1) + k
        term = w_ref[k:k + 1, cols] * buf_ref[start:start + tt, cols]
        acc = term if acc is None else acc + term
    return acc


def _sc_kernel(h_ref, g_ref, win_ref, cw_ref, wout_ref, cin_ref, out_ref, *rest,
               tt, d, chunk, emit_carry):
    if emit_carry:
        cout_ref, xn_ref, cv_ref, bu_ref = rest
    else:
        xn_ref, cv_ref, bu_ref = rest

    @pl.when(pl.program_id(1) == 0)
    def _():
        cv_ref[0:CARRY_ROWS, :] = cin_ref[...]

    x = h_ref[0]
    xn_ref[...] = _rms_scale(x, g_ref[...]).astype(BF16)
    for j in range(d // chunk):
        cols = slice(j * chunk, (j + 1) * chunk)

        def proj(part):
            return jnp.dot(xn_ref[...], win_ref[:, part * d + j * chunk:part * d + (j + 1) * chunk],
                           preferred_element_type=F32)

        cv_ref[CARRY_ROWS:CARRY_ROWS + tt, cols] = proj(1) * proj(2)
        u = _causal_conv(cv_ref, cw_ref, tt, cols, 3)
        bu_ref[:, cols] = (proj(0) * u).astype(BF16)
    cv_ref[0:CARRY_ROWS, :] = cv_ref[tt:tt + CARRY_ROWS, :]
    if emit_carry:
        cout_ref[0] = cv_ref[0:CARRY_ROWS, :]
    out_ref[0] = x + jnp.dot(bu_ref[...], wout_ref[...], preferred_element_type=F32)


def _short_conv_layer(h, g, w_in, conv_w, w_out, carry_in, *, tt, emit_carry):
    bsz, t_len, d = h.shape
    grid = (bsz, t_len // tt)
    const = lambda b, t: (0, 0)
    resident = functools.partial(pl.BlockSpec, index_map=const, pipeline_mode=pl.Buffered(1))
    in_specs = [
        pl.BlockSpec((1, tt, d), lambda b, t: (b, t, 0)),
        resident((1, d)),
        resident((d, 3 * d)),
        resident((3, d)),
        resident((d, d)),
        resident((CARRY_ROWS, d)),
    ]
    out_shape = [jax.ShapeDtypeStruct((bsz, t_len, d), F32)]
    out_specs = [pl.BlockSpec((1, tt, d), lambda b, t: (b, t, 0))]
    if emit_carry:
        out_shape.append(jax.ShapeDtypeStruct((bsz, CARRY_ROWS, d), F32))
        out_specs.append(pl.BlockSpec((1, CARRY_ROWS, d), lambda b, t: (b, 0, 0)))
    res = pl.pallas_call(
        functools.partial(_sc_kernel, tt=tt, d=d, chunk=min(256, d), emit_carry=emit_carry),
        grid=grid, in_specs=in_specs, out_specs=out_specs, out_shape=out_shape,
        scratch_shapes=[pltpu.VMEM((tt, d), BF16),
                        pltpu.VMEM((CARRY_ROWS + tt, d), F32),
                        pltpu.VMEM((tt, d), BF16)],
        compiler_params=pltpu.CompilerParams(dimension_semantics=("arbitrary", "arbitrary"),
                                             vmem_limit_bytes=VMEM_LIMIT_BYTES),
        name="short_conv_mixer",
    )(h, g, w_in, conv_w, w_out, carry_in)
    return res if emit_carry else (res[0], None)


def _rg_kernel(h_ref, g_ref, win_ref, cw_ref, cb_ref, wg_ref, bga_ref, bgx_ref, lam_ref, wout_ref,
               cin_ref, sin_ref, out_ref, *rest, tt, d_rnn, blk, emit_carry):
    if emit_carry:
        cout_ref, sout_ref, xn_ref, r_ref, gate_ref, a_ref, b_ref, y_ref, st_ref = rest
    else:
        xn_ref, r_ref, gate_ref, a_ref, b_ref, y_ref, st_ref = rest

    @pl.when(pl.program_id(1) == 0)
    def _():
        r_ref[0:CARRY_ROWS, :] = cin_ref[...]
        st_ref[...] = sin_ref[...]

    x = h_ref[0]
    xn_ref[...] = _rms_scale(x, g_ref[...]).astype(BF16)

    lam = lam_ref[...]
    neg = -lam
    softplus_neg_lam = jnp.maximum(neg, 0.0) + jnp.log1p(jnp.exp(-jnp.abs(neg)))
    log_a_scale = -RG_C * softplus_neg_lam

    for k in range(d_rnn // blk):
        cols = slice(k * blk, (k + 1) * blk)
        gb = jnp.dot(xn_ref[...], win_ref[:, cols], preferred_element_type=F32)
        inner = 0.7978845608028654 * (gb + 0.044715 * (gb * gb * gb))
        gate_ref[:, cols] = 0.5 * gb * (1.0 + jnp.tanh(inner))
        r_ref[CARRY_ROWS:CARRY_ROWS + tt, cols] = jnp.dot(
            xn_ref[...], win_ref[:, d_rnn + k * blk:d_rnn + (k + 1) * blk], preferred_element_type=F32)
        u = _causal_conv(r_ref, cw_ref, tt, cols, 4) + cb_ref[:, cols]
        ri = jnp.dot(u.astype(BF16), wg_ref[k], preferred_element_type=F32)
        r = _sigmoid(ri[:, :blk] + bga_ref[:, cols])
        i = _sigmoid(ri[:, blk:] + bgx_ref[:, cols])
        log_a = log_a_scale[:, cols] * r
        a = jnp.exp(log_a)
        a_ref[:, cols] = a
        b_ref[:, cols] = jnp.sqrt(-jnp.tanh(log_a) * (a * a + 1.0)) * (i * u)
    r_ref[0:CARRY_ROWS, :] = r_ref[tt:tt + CARRY_ROWS, :]

    n_groups = tt // SUBLANES
    a3 = a_ref[...].reshape(n_groups, SUBLANES, d_rnn)
    b3 = b_ref[...].reshape(n_groups, SUBLANES, d_rnn)
    row = lax.broadcasted_iota(jnp.int32, (n_groups, SUBLANES, d_rnn), 1)
    shift = 1
    while shift < SUBLANES:
        keep = row >= shift
        a_prev = jnp.where(keep, pltpu.roll(a3, shift, axis=1), 1.0)
        b_prev = jnp.where(keep, pltpu.roll(b3, shift, axis=1), 0.0)
        b3 = a3 * b_prev + b3
        a3 = a3 * a_prev
        shift *= 2
    a_ref[...] = a3.reshape(tt, d_rnn)
    b_ref[...] = b3.reshape(tt, d_rnn)

    def pair_step(pidx, state):
        base = pl.multiple_of(pidx * (2 * SUBLANES), 2 * SUBLANES)
        parts = []
        for half in range(2):
            rows = pl.ds(base + half * SUBLANES, SUBLANES)
            hs = a_ref[rows, :] * state + b_ref[rows, :]
            state = hs[SUBLANES - 1:SUBLANES, :]
            parts.append(hs)
        both = pl.ds(base, 2 * SUBLANES)
        y_ref[both, :] = (jnp.concatenate(parts, axis=0) * gate_ref[both, :]).astype(BF16)
        return state

    st_ref[...] = lax.fori_loop(0, n_groups // 2, pair_step, st_ref[...])
    if emit_carry:
        cout_ref[0] = r_ref[0:CARRY_ROWS, :]
        sout_ref[0] = st_ref[...]
    out_ref[0] = x + jnp.dot(y_ref[...], wout_ref[...], preferred_element_type=F32)


def _rglru_layer(h, g, w_in, conv_w, conv_b, w_gate, b_gate_a, b_gate_x, lam, w_out,
                 carry_in, state_in, *, tt, emit_carry):
    bsz, t_len, d = h.shape
    d_rnn = w_out.shape[0]
    n_blk, blk, _ = w_gate.shape
    grid = (bsz, t_len // tt)
    const2 = lambda b, t: (0, 0)
    resident = functools.partial(pl.BlockSpec, index_map=const2, pipeline_mode=pl.Buffered(1))
    in_specs = [
        pl.BlockSpec((1, tt, d), lambda b, t: (b, t, 0)),
        resident((1, d)),
        resident((d, 2 * d_rnn)),
        resident((4, d_rnn)),
        resident((1, d_rnn)),
        pl.BlockSpec((n_blk, blk, 2 * blk), lambda b, t: (0, 0, 0), pipeline_mode=pl.Buffered(1)),
        resident((1, d_rnn)),
        resident((1, d_rnn)),
        resident((1, d_rnn)),
        resident((d_rnn, d)),
        resident((CARRY_ROWS, d_rnn)),
        resident((1, d_rnn)),
    ]
    out_shape = [jax.ShapeDtypeStruct((bsz, t_len, d), F32)]
    out_specs = [pl.BlockSpec((1, tt, d), lambda b, t: (b, t, 0))]
    if emit_carry:
        out_shape += [jax.ShapeDtypeStruct((bsz, CARRY_ROWS, d_rnn), F32),
                      jax.ShapeDtypeStruct((bsz, 1, d_rnn), F32)]
        out_specs += [pl.BlockSpec((1, CARRY_ROWS, d_rnn), lambda b, t: (b, 0, 0)),
                      pl.BlockSpec((1, 1, d_rnn), lambda b, t: (b, 0, 0))]
    res = pl.pallas_call(
        functools.partial(_rg_kernel, tt=tt, d_rnn=d_rnn, blk=blk, emit_carry=emit_carry),
        grid=grid, in_specs=in_specs, out_specs=out_specs, out_shape=out_shape,
        scratch_shapes=[pltpu.VMEM((tt, d), BF16),
                        pltpu.VMEM((CARRY_ROWS + tt, d_rnn), F32),
                        pltpu.VMEM((tt, d_rnn), F32),
                        pltpu.VMEM((tt, d_rnn), F32),
                        pltpu.VMEM((tt, d_rnn), F32),
                        pltpu.VMEM((tt, d_rnn), BF16),
                        pltpu.VMEM((1, d_rnn), F32)],
        compiler_params=pltpu.CompilerParams(dimension_semantics=("arbitrary", "arbitrary"),
                                             vmem_limit_bytes=VMEM_LIMIT_BYTES),
        name="rglru_mixer",
    )(h, g, w_in, conv_w, conv_b, w_gate, b_gate_a, b_gate_x, lam, w_out, carry_in, state_in)
    return res if emit_carry else (res[0], None, None)


def _ffn_kernel(h_ref, g_ref, wup_ref, cw_ref, wdn_ref, cin_ref, gf_ref, out_ref, *rest,
                tt, d_ff, chunk, final_norm, emit_carry):
    if emit_carry:
        cout_ref, xn_ref, up_ref, act_ref = rest
    else:
        xn_ref, up_ref, act_ref = rest

    @pl.when(pl.program_id(1) == 0)
    def _():
        up_ref[0:CARRY_ROWS, :] = cin_ref[...]

    x = h_ref[0]
    xn_ref[...] = _rms_scale(x, g_ref[...]).astype(BF16)
    for j in range(d_ff // chunk):
        g_cols = slice(j * chunk, (j + 1) * chunk)
        v_cols = slice(d_ff + j * chunk, d_ff + (j + 1) * chunk)
        for cols in (g_cols, v_cols):
            up_ref[CARRY_ROWS:CARRY_ROWS + tt, cols] = jnp.dot(
                xn_ref[...], wup_ref[:, cols], preferred_element_type=F32)
        gq = _causal_conv(up_ref, cw_ref, tt, g_cols, 3)
        vq = _causal_conv(up_ref, cw_ref, tt, v_cols, 3)
        act_ref[:, g_cols] = (gq * _sigmoid(gq) * vq).astype(BF16)
    up_ref[0:CARRY_ROWS, :] = up_ref[tt:tt + CARRY_ROWS, :]
    if emit_carry:
        cout_ref[0] = up_ref[0:CARRY_ROWS, :]
    o = x + jnp.dot(act_ref[...], wdn_ref[...], preferred_element_type=F32)
    if final_norm:
        o = _rms_scale(o, gf_ref[...])
    out_ref[0] = o


def _ffn_layer(h, g, w_up, conv_w, w_down, carry_in, g_final, *, tt, final_norm, emit_carry):
    bsz, t_len, d = h.shape
    d_ff = w_down.shape[0]
    grid = (bsz, t_len // tt)
    const = lambda b, t: (0, 0)
    resident = functools.partial(pl.BlockSpec, index_map=const, pipeline_mode=pl.Buffered(1))
    in_specs = [
        pl.BlockSpec((1, tt, d), lambda b, t: (b, t, 0)),
        resident((1, d)),
        resident((d, 2 * d_ff)),
        resident((3, 2 * d_ff)),
        resident((d_ff, d)),
        resident((CARRY_ROWS, 2 * d_ff)),
        resident((1, d)),
    ]
    out_shape = [jax.ShapeDtypeStruct((bsz, t_len, d), F32)]
    out_specs = [pl.BlockSpec((1, tt, d), lambda b, t: (b, t, 0))]
    if emit_carry:
        out_shape.append(jax.ShapeDtypeStruct((bsz, CARRY_ROWS, 2 * d_ff), F32))
        out_specs.append(pl.BlockSpec((1, CARRY_ROWS, 2 * d_ff), lambda b, t: (b, 0, 0)))
    res = pl.pallas_call(
        functools.partial(_ffn_kernel, tt=tt, d_ff=d_ff, chunk=2 * LANES, final_norm=final_norm,
                          emit_carry=emit_carry),
        grid=grid, in_specs=in_specs, out_specs=out_specs, out_shape=out_shape,
        scratch_shapes=[pltpu.VMEM((tt, d), BF16),
                        pltpu.VMEM((CARRY_ROWS + tt, 2 * d_ff), F32),
                        pltpu.VMEM((tt, d_ff), BF16)],
        compiler_params=pltpu.CompilerParams(dimension_semantics=("arbitrary", "arbitrary"),
                                             vmem_limit_bytes=VMEM_LIMIT_BYTES),
        name="conv_gated_mlp",
    )(h, g, w_up, conv_w, w_down, carry_in, g_final)
    return res if emit_carry else (res[0], None)


def _trunk(h, params, carries, *, tt, emit_carry):
    p = params
    new = {}
    h, new["sc"] = _short_conv_layer(h, p["g_mix0"], p["sc_w_in"], p["sc_conv_w"], p["sc_w_out"],
                                     carries["sc"], tt=tt, emit_carry=emit_carry)
    h, new["ffn0"] = _ffn_layer(h, p["g_ffn0"], p["ffn_w_up0"], p["ffn_conv_w0"], p["ffn_w_down0"],
                                carries["ffn0"], p["g_final"], tt=tt, final_norm=False,
                                emit_carry=emit_carry)
    h, new["rg_conv"], new["rg_state"] = _rglru_layer(
        h, p["g_mix1"], p["rg_w_in"], p["rg_conv_w"], p["rg_conv_b"], p["rg_w_gate"],
        p["rg_b_gate_a"], p["rg_b_gate_x"], p["rg_lambda"], p["rg_w_out"],
        carries["rg_conv"], carries["rg_state"], tt=tt, emit_carry=emit_carry)
    h, new["ffn1"] = _ffn_layer(h, p["g_ffn1"], p["ffn_w_up1"], p["ffn_conv_w1"], p["ffn_w_down1"],
                                carries["ffn1"], p["g_final"], tt=tt, final_norm=True,
                                emit_carry=emit_carry)
    return h, new


def kernel(x, meta_tokens, norm_mix_g, norm_ffn_g, final_norm_g, sc_w_in, sc_conv_w, sc_w_out,
           rg_w_in, rg_conv_w, rg_conv_b, rg_w_gate_a, rg_b_gate_a, rg_w_gate_x, rg_b_gate_x,
           rg_lambda, rg_w_out, ffn_w_up, ffn_conv_w, ffn_w_down):
    bsz, seq, d = x.shape
    n_meta = meta_tokens.shape[0]
    d_rnn = rg_w_out.shape[1]
    d_ff = ffn_w_down.shape[1]
    assert norm_mix_g.shape[0] == 2 and sc_w_in.shape[0] == 1 and rg_w_in.shape[0] == 1
    assert seq % TIME_TILE == 0 and n_meta % SUBLANES == 0 and n_meta >= CARRY_ROWS

    row = lambda v: v.reshape(1, -1).astype(F32)
    params = {
        "g_mix0": row(norm_mix_g[0]), "g_mix1": row(norm_mix_g[1]),
        "g_ffn0": row(norm_ffn_g[0]), "g_ffn1": row(norm_ffn_g[1]), "g_final": row(final_norm_g),
        "sc_w_in": sc_w_in[0].astype(BF16), "sc_conv_w": sc_conv_w[0], "sc_w_out": sc_w_out[0].astype(BF16),
        "rg_w_in": rg_w_in[0].astype(BF16), "rg_conv_w": rg_conv_w[0], "rg_conv_b": row(rg_conv_b[0]),
        "rg_w_gate": jnp.concatenate([rg_w_gate_a[0], rg_w_gate_x[0]], axis=-1).astype(BF16),
        "rg_b_gate_a": row(rg_b_gate_a[0]), "rg_b_gate_x": row(rg_b_gate_x[0]),
        "rg_lambda": row(rg_lambda[0]), "rg_w_out": rg_w_out[0].astype(BF16),
        "ffn_w_up0": ffn_w_up[0].astype(BF16), "ffn_conv_w0": ffn_conv_w[0], "ffn_w_down0": ffn_w_down[0].astype(BF16),
        "ffn_w_up1": ffn_w_up[1].astype(BF16), "ffn_conv_w1": ffn_conv_w[1], "ffn_w_down1": ffn_w_down[1].astype(BF16),
    }
    zero_carries = {
        "sc": jnp.zeros((CARRY_ROWS, d), F32),
        "ffn0": jnp.zeros((CARRY_ROWS, 2 * d_ff), F32),
        "rg_conv": jnp.zeros((CARRY_ROWS, d_rnn), F32),
        "rg_state": jnp.zeros((1, d_rnn), F32),
        "ffn1": jnp.zeros((CARRY_ROWS, 2 * d_ff), F32),
    }
    _, meta_carries = _trunk(meta_tokens.astype(x.dtype)[None], params, zero_carries,
                             tt=n_meta, emit_carry=True)
    meta_carries = {k: v[0] for k, v in meta_carries.items()}
    out, _ = _trunk(x, params, meta_carries, tt=TIME_TILE, emit_carry=False)
    return out
```

```python
import functools

import jax
import jax.numpy as jnp
from jax import lax
from jax.experimental import pallas as pl
from jax.experimental.pallas import tpu as pltpu

F32 = jnp.float32
BF16 = jnp.bfloat16

RMS_EPS = 1e-6
RG_C = 8.0
SUBLANES = 8
LANES = 128
VMEM_LIMIT_BYTES = 56 * 1024 * 1024
TIME_TILE = 512


def _rms_scale(x, g):
    var = jnp.mean(x * x, axis=-1, keepdims=True)
    return x * lax.rsqrt(var + RMS_EPS) * g


def _sigmoid(x):
    return 1.0 / (1.0 + jnp.exp(-x))


def _to_interleaved(x):
    tt, d = x.shape
    return pltpu.einshape("std->tsd", x.reshape(SUBLANES, tt // SUBLANES, d)).reshape(tt, d)


def _from_interleaved(x):
    tt, d = x.shape
    return pltpu.einshape("tsd->std", x.reshape(tt // SUBLANES, SUBLANES, d)).reshape(tt, d)


def _sublane_ids(ncols):
    return lax.broadcasted_iota(jnp.int32, (SUBLANES, ncols), 0)


def _conv_head(width, interleaved):
    return SUBLANES * (width - 1) if interleaved else SUBLANES


def _conv_init(buf_ref, hist_ref, cin_ref, width, interleaved):
    if interleaved:
        for m in range(1, width):
            hist_ref[m - 1] = pltpu.roll(cin_ref[...], m - 1, axis=0) if m > 1 else cin_ref[...]
    else:
        buf_ref[0:SUBLANES, :] = cin_ref[...]


def _conv_fill_head(buf_ref, hist_ref, cols, ncols, width, tt, interleaved):
    if not interleaved:
        return
    head = _conv_head(width, True)
    seg_len = tt // SUBLANES
    last = _sublane_ids(ncols) == SUBLANES - 1
    for m in range(1, width):
        src = head + (seg_len - m) * SUBLANES
        grp = buf_ref[src:src + SUBLANES, cols]
        prev = hist_ref[m - 1, :, cols]
        dst = head - m * SUBLANES
        buf_ref[dst:dst + SUBLANES, cols] = pltpu.roll(jnp.where(last, prev, grp), 1, axis=0)
        hist_ref[m - 1, :, cols] = grp


def _conv_finish(buf_ref, tt, interleaved):
    if not interleaved:
        buf_ref[0:SUBLANES, :] = buf_ref[tt:tt + SUBLANES, :]


def _causal_conv(buf_ref, w_ref, tt, cols, width, interleaved):
    head = _conv_head(width, interleaved)
    step = SUBLANES if interleaved else 1
    acc = None
    for k in range(width):
        start = head - (width - 1 - k) * step
        term = w_ref[k:k + 1, cols] * buf_ref[start:start + tt, cols]
        acc = term if acc is None else acc + term
    return acc


def _conv_scratch(width, tt, ncols, interleaved):
    shapes = [pltpu.VMEM((_conv_head(width, interleaved) + tt, ncols), F32)]
    if interleaved:
        shapes.append(pltpu.VMEM((width - 1, SUBLANES, ncols), F32))
    return shapes


def _sc_kernel(*refs, tt, d, chunk, interleaved, emit_carry):
    h_ref, g_ref, win_ref, cw_ref, wout_ref, cin_ref, out_ref = refs[:7]
    rest = list(refs[7:])
    cout_ref = rest.pop(0) if emit_carry else None
    xn_ref, bu_ref, cv_ref = rest[:3]
    hist_ref = rest[3] if interleaved else None
    head = _conv_head(3, interleaved)

    @pl.when(pl.program_id(1) == 0)
    def _():
        _conv_init(cv_ref, hist_ref, cin_ref, 3, interleaved)

    x = h_ref[0]
    if interleaved:
        x = _to_interleaved(x)
    xn_ref[...] = _rms_scale(x, g_ref[...]).astype(BF16)
    for j in range(d // chunk):
        cols = slice(j * chunk, (j + 1) * chunk)

        def proj(part):
            return jnp.dot(xn_ref[...], win_ref[:, part * d + j * chunk:part * d + (j + 1) * chunk],
                           preferred_element_type=F32)

        cv_ref[head:head + tt, cols] = proj(1) * proj(2)
        _conv_fill_head(cv_ref, hist_ref, cols, chunk, 3, tt, interleaved)
        u = _causal_conv(cv_ref, cw_ref, tt, cols, 3, interleaved)
        bu_ref[:, cols] = (proj(0) * u).astype(BF16)
    _conv_finish(cv_ref, tt, interleaved)
    if emit_carry:
        cout_ref[0] = cv_ref[0:SUBLANES, :]
    out_ref[0] = x + jnp.dot(bu_ref[...], wout_ref[...], preferred_element_type=F32)


def _resident(shape):
    return pl.BlockSpec(shape, lambda b, t: (0,) * len(shape), pipeline_mode=pl.Buffered(1))


def _tile_spec(tt, d):
    return pl.BlockSpec((1, tt, d), lambda b, t: (b, t, 0))


def _carry_out(bsz, rows, ncols):
    return (jax.ShapeDtypeStruct((bsz, rows, ncols), F32),
            pl.BlockSpec((1, rows, ncols), lambda b, t: (b, 0, 0)))


_COMPILER_PARAMS = pltpu.CompilerParams(dimension_semantics=("arbitrary", "arbitrary"),
                                        vmem_limit_bytes=VMEM_LIMIT_BYTES)


def _short_conv_layer(h, g, w_in, conv_w, w_out, carry_in, *, tt, interleaved, emit_carry):
    assert not (interleaved and emit_carry)
    bsz, t_len, d = h.shape
    in_specs = [_tile_spec(tt, d), _resident((1, d)), _resident((d, 3 * d)), _resident((3, d)),
                _resident((d, d)), _resident((SUBLANES, d))]
    out_shape, out_specs = [jax.ShapeDtypeStruct((bsz, t_len, d), F32)], [_tile_spec(tt, d)]
    if emit_carry:
        shape, spec = _carry_out(bsz, SUBLANES, d)
        out_shape.append(shape)
        out_specs.append(spec)
    res = pl.pallas_call(
        functools.partial(_sc_kernel, tt=tt, d=d, chunk=min(2 * LANES, d), interleaved=interleaved,
                          emit_carry=emit_carry),
        grid=(bsz, t_len // tt), in_specs=in_specs, out_specs=out_specs, out_shape=out_shape,
        scratch_shapes=[pltpu.VMEM((tt, d), BF16), pltpu.VMEM((tt, d), BF16)]
        + _conv_scratch(3, tt, d, interleaved),
        compiler_params=_COMPILER_PARAMS, name="short_conv_mixer",
    )(h, g, w_in, conv_w, w_out, carry_in)
    return res if emit_carry else (res[0], None)


def _group_scan(a, b):
    axis = a.ndim - 2
    row = lax.broadcasted_iota(jnp.int32, a.shape, axis)
    shift = 1
    while shift < SUBLANES:
        keep = row >= shift
        a_prev = jnp.where(keep, pltpu.roll(a, shift, axis=axis), 1.0)
        b_prev = jnp.where(keep, pltpu.roll(b, shift, axis=axis), 0.0)
        b = a * b_prev + b
        a = a * a_prev
        shift *= 2
    return a, b


def _scan_chunk_interleaved(a_ref, b_ref, cols, ncols, state, tt):
    seg_len = tt // SUBLANES
    hs = b_ref[0:SUBLANES, cols]
    prod = a_ref[0:SUBLANES, cols]
    for tau in range(1, seg_len):
        rows = slice(tau * SUBLANES, (tau + 1) * SUBLANES)
        a = a_ref[rows, cols]
        hs = a * hs + b_ref[rows, cols]
        prod = a * prod
        b_ref[rows, cols] = hs
        a_ref[rows, cols] = prod
    prod_inc, hs_inc = _group_scan(prod, hs)
    after = prod_inc * state + hs_inc
    entering = jnp.where(_sublane_ids(ncols) == 0, state, pltpu.roll(after, 1, axis=0))
    return entering, after[SUBLANES - 1:SUBLANES, :]


def _rg_kernel(*refs, tt, d_rnn, blk, interleaved, emit_carry):
    (h_ref, g_ref, win_ref, cw_ref, cb_ref, wg_ref, bga_ref, bgx_ref, lam_ref, wout_ref,
     cin_ref, sin_ref, out_ref) = refs[:13]
    rest = list(refs[13:])
    cout_ref = rest.pop(0) if emit_carry else None
    sout_ref = rest.pop(0) if emit_carry else None
    xn_ref, gate_ref, a_ref, b_ref, y_ref, st_ref, r_ref = rest[:7]
    hist_ref = rest[7] if interleaved else None
    head = _conv_head(4, interleaved)

    @pl.when(pl.program_id(1) == 0)
    def _():
        _conv_init(r_ref, hist_ref, cin_ref, 4, interleaved)
        st_ref[...] = sin_ref[...]

    x = h_ref[0]
    xn_ref[...] = _rms_scale(x, g_ref[...]).astype(BF16)

    neg = -lam_ref[...]
    softplus_neg_lam = jnp.maximum(neg, 0.0) + jnp.log1p(jnp.exp(-jnp.abs(neg)))
    log_a_scale = -RG_C * softplus_neg_lam

    for k in range(d_rnn // blk):
        cols = slice(k * blk, (k + 1) * blk)
        gb = jnp.dot(xn_ref[...], win_ref[:, cols], preferred_element_type=F32)
        inner = 0.7978845608028654 * (gb + 0.044715 * (gb * gb * gb))
        gate_ref[:, cols] = 0.5 * gb * (1.0 + jnp.tanh(inner))
        r_ref[head:head + tt, cols] = jnp.dot(
            xn_ref[...], win_ref[:, d_rnn + k * blk:d_rnn + (k + 1) * blk], preferred_element_type=F32)
        _conv_fill_head(r_ref, hist_ref, cols, blk, 4, tt, interleaved)
        u = _causal_conv(r_ref, cw_ref, tt, cols, 4, interleaved) + cb_ref[:, cols]
        ri = jnp.dot(u.astype(BF16), wg_ref[k], preferred_element_type=F32)
        r = _sigmoid(ri[:, :blk] + bga_ref[:, cols])
        i = _sigmoid(ri[:, blk:] + bgx_ref[:, cols])
        log_a = log_a_scale[:, cols] * r
        a = jnp.exp(log_a)
        a_ref[:, cols] = a
        b_ref[:, cols] = jnp.sqrt(-jnp.tanh(log_a) * (a * a + 1.0)) * (i * u)
        if interleaved:
            entering, st_ref[:, cols] = _scan_chunk_interleaved(a_ref, b_ref, cols, blk,
                                                                st_ref[:, cols], tt)
            shape3 = (tt // SUBLANES, SUBLANES, blk)
            hs = b_ref[:, cols].reshape(shape3) + a_ref[:, cols].reshape(shape3) * entering[None]
            y_ref[:, cols] = (hs.reshape(tt, blk) * gate_ref[:, cols]).astype(BF16)
    _conv_finish(r_ref, tt, interleaved)

    if not interleaved:
        n_groups = tt // SUBLANES
        a3, b3 = _group_scan(a_ref[...].reshape(n_groups, SUBLANES, d_rnn),
                             b_ref[...].reshape(n_groups, SUBLANES, d_rnn))
        a_ref[...] = a3.reshape(tt, d_rnn)
        b_ref[...] = b3.reshape(tt, d_rnn)

        def pair_step(pidx, state):
            base = pl.multiple_of(pidx * (2 * SUBLANES), 2 * SUBLANES)
            parts = []
            for half in range(2):
                rows = pl.ds(base + half * SUBLANES, SUBLANES)
                hs = a_ref[rows, :] * state + b_ref[rows, :]
                state = hs[SUBLANES - 1:SUBLANES, :]
                parts.append(hs)
            both = pl.ds(base, 2 * SUBLANES)
            y_ref[both, :] = (jnp.concatenate(parts, axis=0) * gate_ref[both, :]).astype(BF16)
            return state

        st_ref[...] = lax.fori_loop(0, n_groups // 2, pair_step, st_ref[...])
    if emit_carry:
        cout_ref[0] = r_ref[0:SUBLANES, :]
        sout_ref[0] = st_ref[...]
    out_ref[0] = x + jnp.dot(y_ref[...], wout_ref[...], preferred_element_type=F32)


def _rglru_layer(h, g, w_in, conv_w, conv_b, w_gate, b_gate_a, b_gate_x, lam, w_out,
                 carry_in, state_in, *, tt, interleaved, emit_carry):
    assert not (interleaved and emit_carry)
    bsz, t_len, d = h.shape
    d_rnn = w_out.shape[0]
    n_blk, blk, _ = w_gate.shape
    in_specs = [_tile_spec(tt, d), _resident((1, d)), _resident((d, 2 * d_rnn)), _resident((4, d_rnn)),
                _resident((1, d_rnn)), _resident((n_blk, blk, 2 * blk)), _resident((1, d_rnn)),
                _resident((1, d_rnn)), _resident((1, d_rnn)), _resident((d_rnn, d)),
                _resident((SUBLANES, d_rnn)), _resident((1, d_rnn))]
    out_shape, out_specs = [jax.ShapeDtypeStruct((bsz, t_len, d), F32)], [_tile_spec(tt, d)]
    if emit_carry:
        for rows in (SUBLANES, 1):
            shape, spec = _carry_out(bsz, rows, d_rnn)
            out_shape.append(shape)
            out_specs.append(spec)
    res = pl.pallas_call(
        functools.partial(_rg_kernel, tt=tt, d_rnn=d_rnn, blk=blk, interleaved=interleaved,
                          emit_carry=emit_carry),
        grid=(bsz, t_len // tt), in_specs=in_specs, out_specs=out_specs, out_shape=out_shape,
        scratch_shapes=[pltpu.VMEM((tt, d), BF16),
                        pltpu.VMEM((tt, d_rnn), F32),
                        pltpu.VMEM((tt, d_rnn), F32),
                        pltpu.VMEM((tt, d_rnn), F32),
                        pltpu.VMEM((tt, d_rnn), BF16),
                        pltpu.VMEM((1, d_rnn), F32)] + _conv_scratch(4, tt, d_rnn, interleaved),
        compiler_params=_COMPILER_PARAMS, name="rglru_mixer",
    )(h, g, w_in, conv_w, conv_b, w_gate, b_gate_a, b_gate_x, lam, w_out, carry_in, state_in)
    return res if emit_carry else (res[0], None, None)


def _ffn_kernel(*refs, tt, d_ff, chunk, final_norm, interleaved, emit_carry):
    h_ref, g_ref, wup_ref, cw_ref, wdn_ref, cin_ref, gf_ref, out_ref = refs[:8]
    rest = list(refs[8:])
    cout_ref = rest.pop(0) if emit_carry else None
    xn_ref, act_ref, up_ref = rest[:3]
    hist_ref = rest[3] if interleaved else None
    head = _conv_head(3, interleaved)

    @pl.when(pl.program_id(1) == 0)
    def _():
        _conv_init(up_ref, hist_ref, cin_ref, 3, interleaved)

    x = h_ref[0]
    xn_ref[...] = _rms_scale(x, g_ref[...]).astype(BF16)
    for j in range(d_ff // chunk):
        g_cols = slice(j * chunk, (j + 1) * chunk)
        v_cols = slice(d_ff + j * chunk, d_ff + (j + 1) * chunk)
        for cols in (g_cols, v_cols):
            up_ref[head:head + tt, cols] = jnp.dot(xn_ref[...], wup_ref[:, cols],
                                                   preferred_element_type=F32)
            _conv_fill_head(up_ref, hist_ref, cols, chunk, 3, tt, interleaved)
        gq = _causal_conv(up_ref, cw_ref, tt, g_cols, 3, interleaved)
        vq = _causal_conv(up_ref, cw_ref, tt, v_cols, 3, interleaved)
        act_ref[:, g_cols] = (gq * _sigmoid(gq) * vq).astype(BF16)
    _conv_finish(up_ref, tt, interleaved)
    if emit_carry:
        cout_ref[0] = up_ref[0:SUBLANES, :]
    o = x + jnp.dot(act_ref[...], wdn_ref[...], preferred_element_type=F32)
    if final_norm:
        o = _rms_scale(o, gf_ref[...])
        if interleaved:
            o = _from_interleaved(o)
    out_ref[0] = o


def _ffn_layer(h, g, w_up, conv_w, w_down, carry_in, g_final, *, tt, final_norm, interleaved,
               emit_carry):
    assert not (interleaved and emit_carry)
    bsz, t_len, d = h.shape
    d_ff = w_down.shape[0]
    in_specs = [_tile_spec(tt, d), _resident((1, d)), _resident((d, 2 * d_ff)),
                _resident((3, 2 * d_ff)), _resident((d_ff, d)), _resident((SUBLANES, 2 * d_ff)),
                _resident((1, d))]
    out_shape, out_specs = [jax.ShapeDtypeStruct((bsz, t_len, d), F32)], [_tile_spec(tt, d)]
    if emit_carry:
        shape, spec = _carry_out(bsz, SUBLANES, 2 * d_ff)
        out_shape.append(shape)
        out_specs.append(spec)
    res = pl.pallas_call(
        functools.partial(_ffn_kernel, tt=tt, d_ff=d_ff, chunk=2 * LANES, final_norm=final_norm,
                          interleaved=interleaved, emit_carry=emit_carry),
        grid=(bsz, t_len // tt), in_specs=in_specs, out_specs=out_specs, out_shape=out_shape,
        scratch_shapes=[pltpu.VMEM((tt, d), BF16), pltpu.VMEM((tt, d_ff), BF16)]
        + _conv_scratch(3, tt, 2 * d_ff, interleaved),
        compiler_params=_COMPILER_PARAMS, name="conv_gated_mlp",
    )(h, g, w_up, conv_w, w_down, carry_in, g_final)
    return res if emit_carry else (res[0], None)


def _trunk(h, params, carries, *, tt, interleaved, emit_carry):
    p = params
    mode = dict(tt=tt, interleaved=interleaved, emit_carry=emit_carry)
    new = {}
    h, new["sc"] = _short_conv_layer(h, p["g_mix0"], p["sc_w_in"], p["sc_conv_w"], p["sc_w_out"],
                                     carries["sc"], **mode)
    h, new["ffn0"] = _ffn_layer(h, p["g_ffn0"], p["ffn_w_up0"], p["ffn_conv_w0"], p["ffn_w_down0"],
                                carries["ffn0"], p["g_final"], final_norm=False, **mode)
    h, new["rg_conv"], new["rg_state"] = _rglru_layer(
        h, p["g_mix1"], p["rg_w_in"], p["rg_conv_w"], p["rg_conv_b"], p["rg_w_gate"],
        p["rg_b_gate_a"], p["rg_b_gate_x"], p["rg_lambda"], p["rg_w_out"],
        carries["rg_conv"], carries["rg_state"], **mode)
    h, new["ffn1"] = _ffn_layer(h, p["g_ffn1"], p["ffn_w_up1"], p["ffn_conv_w1"], p["ffn_w_down1"],
                                carries["ffn1"], p["g_final"], final_norm=True, **mode)
    return h, new


def kernel(x, meta_tokens, norm_mix_g, norm_ffn_g, final_norm_g, sc_w_in, sc_conv_w, sc_w_out,
           rg_w_in, rg_conv_w, rg_conv_b, rg_w_gate_a, rg_b_gate_a, rg_w_gate_x, rg_b_gate_x,
           rg_lambda, rg_w_out, ffn_w_up, ffn_conv_w, ffn_w_down):
    bsz, seq, d = x.shape
    n_meta = meta_tokens.shape[0]
    d_rnn = rg_w_out.shape[1]
    d_ff = ffn_w_down.shape[1]
    assert norm_mix_g.shape[0] == 2 and sc_w_in.shape[0] == 1 and rg_w_in.shape[0] == 1
    assert seq % TIME_TILE == 0 and n_meta % (2 * SUBLANES) == 0

    row = lambda v: v.reshape(1, -1).astype(F32)
    params = {
        "g_mix0": row(norm_mix_g[0]), "g_mix1": row(norm_mix_g[1]),
        "g_ffn0": row(norm_ffn_g[0]), "g_ffn1": row(norm_ffn_g[1]), "g_final": row(final_norm_g),
        "sc_w_in": sc_w_in[0].astype(BF16), "sc_conv_w": sc_conv_w[0], "sc_w_out": sc_w_out[0].astype(BF16),
        "rg_w_in": rg_w_in[0].astype(BF16), "rg_conv_w": rg_conv_w[0], "rg_conv_b": row(rg_conv_b[0]),
        "rg_w_gate": jnp.concatenate([rg_w_gate_a[0], rg_w_gate_x[0]], axis=-1).astype(BF16),
        "rg_b_gate_a": row(rg_b_gate_a[0]), "rg_b_gate_x": row(rg_b_gate_x[0]),
        "rg_lambda": row(rg_lambda[0]), "rg_w_out": rg_w_out[0].astype(BF16),
        "ffn_w_up0": ffn_w_up[0].astype(BF16), "ffn_conv_w0": ffn_conv_w[0], "ffn_w_down0": ffn_w_down[0].astype(BF16),
        "ffn_w_up1": ffn_w_up[1].astype(BF16), "ffn_conv_w1": ffn_conv_w[1], "ffn_w_down1": ffn_w_down[1].astype(BF16),
    }
    zero_carries = {
        "sc": jnp.zeros((SUBLANES, d), F32),
        "ffn0": jnp.zeros((SUBLANES, 2 * d_ff), F32),
        "rg_conv": jnp.zeros((SUBLANES, d_rnn), F32),
        "rg_state": jnp.zeros((1, d_rnn), F32),
        "ffn1": jnp.zeros((SUBLANES, 2 * d_ff), F32),
    }
    _, meta_carries = _trunk(meta_tokens.astype(x.dtype)[None], params, zero_carries,
                             tt=n_meta, interleaved=False, emit_carry=True)
    meta_carries = {k: v[0] for k, v in meta_carries.items()}
    out, _ = _trunk(x, params, meta_carries, tt=TIME_TILE, interleaved=True, emit_carry=False)
    return out
```

```python
import functools

import jax
import jax.numpy as jnp
from jax import lax
from jax.experimental import pallas as pl
from jax.experimental.pallas import tpu as pltpu

F32 = jnp.float32
BF16 = jnp.bfloat16

RMS_EPS = 1e-6
RG_C = 8.0
SUBLANES = 8
LANES = 128
VMEM_LIMIT_BYTES = 56 * 1024 * 1024
TIME_TILE = 512


def _rms_scale(x, g):
    var = jnp.mean(x * x, axis=-1, keepdims=True)
    return x * lax.rsqrt(var + RMS_EPS) * g


def _sigmoid(x):
    return 1.0 / (1.0 + jnp.exp(-x))


def _to_interleaved(x):
    tt, d = x.shape
    return jnp.swapaxes(x.reshape(SUBLANES, tt // SUBLANES, d), 0, 1).reshape(tt, d)


def _from_interleaved(x):
    tt, d = x.shape
    return jnp.swapaxes(x.reshape(tt // SUBLANES, SUBLANES, d), 0, 1).reshape(tt, d)


def _sublane_ids(ncols):
    return lax.broadcasted_iota(jnp.int32, (SUBLANES, ncols), 0)


def _conv_head(width, interleaved):
    return SUBLANES * (width - 1) if interleaved else SUBLANES


def _conv_init(buf_ref, hist_ref, cin_ref, width, interleaved):
    if interleaved:
        for m in range(1, width):
            hist_ref[m - 1] = pltpu.roll(cin_ref[...], m - 1, axis=0) if m > 1 else cin_ref[...]
    else:
        buf_ref[0:SUBLANES, :] = cin_ref[...]


def _conv_fill_head(buf_ref, hist_ref, cols, ncols, width, tt, interleaved):
    if not interleaved:
        return
    head = _conv_head(width, True)
    seg_len = tt // SUBLANES
    last = _sublane_ids(ncols) == SUBLANES - 1
    for m in range(1, width):
        src = head + (seg_len - m) * SUBLANES
        grp = buf_ref[src:src + SUBLANES, cols]
        prev = hist_ref[m - 1, :, cols]
        dst = head - m * SUBLANES
        buf_ref[dst:dst + SUBLANES, cols] = pltpu.roll(jnp.where(last, prev, grp), 1, axis=0)
        hist_ref[m - 1, :, cols] = grp


def _conv_finish(buf_ref, tt, interleaved):
    if not interleaved:
        buf_ref[0:SUBLANES, :] = buf_ref[tt:tt + SUBLANES, :]


def _causal_conv(buf_ref, w_ref, tt, cols, width, interleaved):
    head = _conv_head(width, interleaved)
    step = SUBLANES if interleaved else 1
    acc = None
    for k in range(width):
        start = head - (width - 1 - k) * step
        term = w_ref[k:k + 1, cols] * buf_ref[start:start + tt, cols]
        acc = term if acc is None else acc + term
    return acc


def _conv_scratch(width, tt, ncols, interleaved):
    shapes = [pltpu.VMEM((_conv_head(width, interleaved) + tt, ncols), F32)]
    if interleaved:
        shapes.append(pltpu.VMEM((width - 1, SUBLANES, ncols), F32))
    return shapes


def _sc_kernel(*refs, tt, d, chunk, interleaved, emit_carry):
    h_ref, g_ref, win_ref, cw_ref, wout_ref, cin_ref, out_ref = refs[:7]
    rest = list(refs[7:])
    cout_ref = rest.pop(0) if emit_carry else None
    xn_ref, bu_ref, cv_ref = rest[:3]
    hist_ref = rest[3] if interleaved else None
    head = _conv_head(3, interleaved)

    @pl.when(pl.program_id(1) == 0)
    def _():
        _conv_init(cv_ref, hist_ref, cin_ref, 3, interleaved)

    x = h_ref[0]
    if interleaved:
        x = _to_interleaved(x)
    xn_ref[...] = _rms_scale(x, g_ref[...]).astype(BF16)
    for j in range(d // chunk):
        cols = slice(j * chunk, (j + 1) * chunk)

        def proj(part):
            return jnp.dot(xn_ref[...], win_ref[:, part * d + j * chunk:part * d + (j + 1) * chunk],
                           preferred_element_type=F32)

        cv_ref[head:head + tt, cols] = proj(1) * proj(2)
        _conv_fill_head(cv_ref, hist_ref, cols, chunk, 3, tt, interleaved)
        u = _causal_conv(cv_ref, cw_ref, tt, cols, 3, interleaved)
        bu_ref[:, cols] = (proj(0) * u).astype(BF16)
    _conv_finish(cv_ref, tt, interleaved)
    if emit_carry:
        cout_ref[0] = cv_ref[0:SUBLANES, :]
    out_ref[0] = x + jnp.dot(bu_ref[...], wout_ref[...], preferred_element_type=F32)


def _resident(shape):
    return pl.BlockSpec(shape, lambda b, t: (0,) * len(shape), pipeline_mode=pl.Buffered(1))


def _tile_spec(tt, d):
    return pl.BlockSpec((1, tt, d), lambda b, t: (b, t, 0))


def _carry_out(bsz, rows, ncols):
    return (jax.ShapeDtypeStruct((bsz, rows, ncols), F32),
            pl.BlockSpec((1, rows, ncols), lambda b, t: (b, 0, 0)))


_COMPILER_PARAMS = pltpu.CompilerParams(dimension_semantics=("arbitrary", "arbitrary"),
                                        vmem_limit_bytes=VMEM_LIMIT_BYTES)


def _short_conv_layer(h, g, w_in, conv_w, w_out, carry_in, *, tt, interleaved, emit_carry):
    assert not (interleaved and emit_carry)
    bsz, t_len, d = h.shape
    in_specs = [_tile_spec(tt, d), _resident((1, d)), _resident((d, 3 * d)), _resident((3, d)),
                _resident((d, d)), _resident((SUBLANES, d))]
    out_shape, out_specs = [jax.ShapeDtypeStruct((bsz, t_len, d), F32)], [_tile_spec(tt, d)]
    if emit_carry:
        shape, spec = _carry_out(bsz, SUBLANES, d)
        out_shape.append(shape)
        out_specs.append(spec)
    res = pl.pallas_call(
        functools.partial(_sc_kernel, tt=tt, d=d, chunk=min(2 * LANES, d), interleaved=interleaved,
                          emit_carry=emit_carry),
        grid=(bsz, t_len // tt), in_specs=in_specs, out_specs=out_specs, out_shape=out_shape,
        scratch_shapes=[pltpu.VMEM((tt, d), BF16), pltpu.VMEM((tt, d), BF16)]
        + _conv_scratch(3, tt, d, interleaved),
        compiler_params=_COMPILER_PARAMS, name="short_conv_mixer",
    )(h, g, w_in, conv_w, w_out, carry_in)
    return res if emit_carry else (res[0], None)


def _group_scan(a, b):
    axis = a.ndim - 2
    row = lax.broadcasted_iota(jnp.int32, a.shape, axis)
    shift = 1
    while shift < SUBLANES:
        keep = row >= shift
        a_prev = jnp.where(keep, pltpu.roll(a, shift, axis=axis), 1.0)
        b_prev = jnp.where(keep, pltpu.roll(b, shift, axis=axis), 0.0)
        b = a * b_prev + b
        a = a * a_prev
        shift *= 2
    return a, b


def _scan_chunk_interleaved(a_ref, b_ref, cols, ncols, state, tt):
    seg_len = tt // SUBLANES
    hs = b_ref[0:SUBLANES, cols]
    prod = a_ref[0:SUBLANES, cols]
    for tau in range(1, seg_len):
        rows = slice(tau * SUBLANES, (tau + 1) * SUBLANES)
        a = a_ref[rows, cols]
        hs = a * hs + b_ref[rows, cols]
        prod = a * prod
        b_ref[rows, cols] = hs
        a_ref[rows, cols] = prod
    prod_inc, hs_inc = _group_scan(prod, hs)
    after = prod_inc * state + hs_inc
    entering = jnp.where(_sublane_ids(ncols) == 0, state, pltpu.roll(after, 1, axis=0))
    return entering, after[SUBLANES - 1:SUBLANES, :]


def _rg_kernel(*refs, tt, d_rnn, blk, chunk, interleaved, emit_carry):
    (h_ref, g_ref, win_ref, cw_ref, cb_ref, wg_ref, bga_ref, bgx_ref, lam_ref, wout_ref,
     cin_ref, sin_ref, out_ref) = refs[:13]
    rest = list(refs[13:])
    cout_ref = rest.pop(0) if emit_carry else None
    sout_ref = rest.pop(0) if emit_carry else None
    xn_ref, gate_ref, a_ref, b_ref, y_ref, st_ref, r_ref = rest[:7]
    hist_ref = rest[7] if interleaved else None
    head = _conv_head(4, interleaved)

    @pl.when(pl.program_id(1) == 0)
    def _():
        _conv_init(r_ref, hist_ref, cin_ref, 4, interleaved)
        st_ref[...] = sin_ref[...]

    x = h_ref[0]
    xn_ref[...] = _rms_scale(x, g_ref[...]).astype(BF16)

    neg = -lam_ref[...]
    softplus_neg_lam = jnp.maximum(neg, 0.0) + jnp.log1p(jnp.exp(-jnp.abs(neg)))
    log_a_scale = -RG_C * softplus_neg_lam

    for j in range(d_rnn // chunk):
        cols = slice(j * chunk, (j + 1) * chunk)
        gb = jnp.dot(xn_ref[...], win_ref[:, cols], preferred_element_type=F32)
        inner = 0.7978845608028654 * (gb + 0.044715 * (gb * gb * gb))
        gate_ref[:, cols] = 0.5 * gb * (1.0 + jnp.tanh(inner))
        r_ref[head:head + tt, cols] = jnp.dot(
            xn_ref[...], win_ref[:, d_rnn + j * chunk:d_rnn + (j + 1) * chunk],
            preferred_element_type=F32)
        _conv_fill_head(r_ref, hist_ref, cols, chunk, 4, tt, interleaved)
        u = _causal_conv(r_ref, cw_ref, tt, cols, 4, interleaved) + cb_ref[:, cols]
        gates = [jnp.dot(u[:, q * blk:(q + 1) * blk].astype(BF16), wg_ref[j * (chunk // blk) + q],
                         preferred_element_type=F32) for q in range(chunk // blk)]
        r = _sigmoid(jnp.concatenate([gq[:, :blk] for gq in gates], axis=1) + bga_ref[:, cols])
        i = _sigmoid(jnp.concatenate([gq[:, blk:] for gq in gates], axis=1) + bgx_ref[:, cols])
        log_a = log_a_scale[:, cols] * r
        a = jnp.exp(log_a)
        a_ref[:, cols] = a
        one_minus_a2 = -jnp.tanh(log_a) * (a * a + 1.0)
        mult = jnp.where(one_minus_a2 > 0.0, one_minus_a2 * lax.rsqrt(one_minus_a2), 0.0)
        b_ref[:, cols] = mult * (i * u)
        if interleaved:
            entering, st_ref[:, cols] = _scan_chunk_interleaved(a_ref, b_ref, cols, chunk,
                                                                st_ref[:, cols], tt)
            shape3 = (tt // SUBLANES, SUBLANES, chunk)
            hs = b_ref[:, cols].reshape(shape3) + a_ref[:, cols].reshape(shape3) * entering[None]
            y_ref[:, cols] = (hs.reshape(tt, chunk) * gate_ref[:, cols]).astype(BF16)
    _conv_finish(r_ref, tt, interleaved)

    if not interleaved:
        n_groups = tt // SUBLANES
        a3, b3 = _group_scan(a_ref[...].reshape(n_groups, SUBLANES, d_rnn),
                             b_ref[...].reshape(n_groups, SUBLANES, d_rnn))
        a_ref[...] = a3.reshape(tt, d_rnn)
        b_ref[...] = b3.reshape(tt, d_rnn)

        def pair_step(pidx, state):
            base = pl.multiple_of(pidx * (2 * SUBLANES), 2 * SUBLANES)
            parts = []
            for half in range(2):
                rows = pl.ds(base + half * SUBLANES, SUBLANES)
                hs = a_ref[rows, :] * state + b_ref[rows, :]
                state = hs[SUBLANES - 1:SUBLANES, :]
                parts.append(hs)
            both = pl.ds(base, 2 * SUBLANES)
            y_ref[both, :] = (jnp.concatenate(parts, axis=0) * gate_ref[both, :]).astype(BF16)
            return state

        st_ref[...] = lax.fori_loop(0, n_groups // 2, pair_step, st_ref[...])
    if emit_carry:
        cout_ref[0] = r_ref[0:SUBLANES, :]
        sout_ref[0] = st_ref[...]
    out_ref[0] = x + jnp.dot(y_ref[...], wout_ref[...], preferred_element_type=F32)


def _rglru_layer(h, g, w_in, conv_w, conv_b, w_gate, b_gate_a, b_gate_x, lam, w_out,
                 carry_in, state_in, *, tt, interleaved, emit_carry):
    assert not (interleaved and emit_carry)
    bsz, t_len, d = h.shape
    d_rnn = w_out.shape[0]
    n_blk, blk, _ = w_gate.shape
    in_specs = [_tile_spec(tt, d), _resident((1, d)), _resident((d, 2 * d_rnn)), _resident((4, d_rnn)),
                _resident((1, d_rnn)), _resident((n_blk, blk, 2 * blk)), _resident((1, d_rnn)),
                _resident((1, d_rnn)), _resident((1, d_rnn)), _resident((d_rnn, d)),
                _resident((SUBLANES, d_rnn)), _resident((1, d_rnn))]
    out_shape, out_specs = [jax.ShapeDtypeStruct((bsz, t_len, d), F32)], [_tile_spec(tt, d)]
    if emit_carry:
        for rows in (SUBLANES, 1):
            shape, spec = _carry_out(bsz, rows, d_rnn)
            out_shape.append(shape)
            out_specs.append(spec)
    res = pl.pallas_call(
        functools.partial(_rg_kernel, tt=tt, d_rnn=d_rnn, blk=blk, chunk=2 * LANES,
                          interleaved=interleaved, emit_carry=emit_carry),
        grid=(bsz, t_len // tt), in_specs=in_specs, out_specs=out_specs, out_shape=out_shape,
        scratch_shapes=[pltpu.VMEM((tt, d), BF16),
                        pltpu.VMEM((tt, d_rnn), F32),
                        pltpu.VMEM((tt, d_rnn), F32),
                        pltpu.VMEM((tt, d_rnn), F32),
                        pltpu.VMEM((tt, d_rnn), BF16),
                        pltpu.VMEM((1, d_rnn), F32)] + _conv_scratch(4, tt, d_rnn, interleaved),
        compiler_params=_COMPILER_PARAMS, name="rglru_mixer",
    )(h, g, w_in, conv_w, conv_b, w_gate, b_gate_a, b_gate_x, lam, w_out, carry_in, state_in)
    return res if emit_carry else (res[0], None, None)


def _ffn_kernel(*refs, tt, d_ff, chunk, final_norm, interleaved, emit_carry):
    h_ref, g_ref, wup_ref, cw_ref, wdn_ref, cin_ref, gf_ref, out_ref = refs[:8]
    rest = list(refs[8:])
    cout_ref = rest.pop(0) if emit_carry else None
    xn_ref, act_ref, up_ref = rest[:3]
    hist_ref = rest[3] if interleaved else None
    head = _conv_head(3, interleaved)

    @pl.when(pl.program_id(1) == 0)
    def _():
        _conv_init(up_ref, hist_ref, cin_ref, 3, interleaved)

    x = h_ref[0]
    xn_ref[...] = _rms_scale(x, g_ref[...]).astype(BF16)
    for j in range(d_ff // chunk):
        g_cols = slice(j * chunk, (j + 1) * chunk)
        v_cols = slice(d_ff + j * chunk, d_ff + (j + 1) * chunk)
        for cols in (g_cols, v_cols):
            up_ref[head:head + tt, cols] = jnp.dot(xn_ref[...], wup_ref[:, cols],
                                                   preferred_element_type=F32)
            _conv_fill_head(up_ref, hist_ref, cols, chunk, 3, tt, interleaved)
        gq = _causal_conv(up_ref, cw_ref, tt, g_cols, 3, interleaved)
        vq = _causal_conv(up_ref, cw_ref, tt, v_cols, 3, interleaved)
        act_ref[:, g_cols] = (gq * _sigmoid(gq) * vq).astype(BF16)
    _conv_finish(up_ref, tt, interleaved)
    if emit_carry:
        cout_ref[0] = up_ref[0:SUBLANES, :]
    o = x + jnp.dot(act_ref[...], wdn_ref[...], preferred_element_type=F32)
    if final_norm:
        o = _rms_scale(o, gf_ref[...])
        if interleaved:
            o = _from_interleaved(o)
    out_ref[0] = o


def _ffn_layer(h, g, w_up, conv_w, w_down, carry_in, g_final, *, tt, final_norm, interleaved,
               emit_carry):
    assert not (interleaved and emit_carry)
    bsz, t_len, d = h.shape
    d_ff = w_down.shape[0]
    in_specs = [_tile_spec(tt, d), _resident((1, d)), _resident((d, 2 * d_ff)),
                _resident((3, 2 * d_ff)), _resident((d_ff, d)), _resident((SUBLANES, 2 * d_ff)),
                _resident((1, d))]
    out_shape, out_specs = [jax.ShapeDtypeStruct((bsz, t_len, d), F32)], [_tile_spec(tt, d)]
    if emit_carry:
        shape, spec = _carry_out(bsz, SUBLANES, 2 * d_ff)
        out_shape.append(shape)
        out_specs.append(spec)
    res = pl.pallas_call(
        functools.partial(_ffn_kernel, tt=tt, d_ff=d_ff, chunk=2 * LANES, final_norm=final_norm,
                          interleaved=interleaved, emit_carry=emit_carry),
        grid=(bsz, t_len // tt), in_specs=in_specs, out_specs=out_specs, out_shape=out_shape,
        scratch_shapes=[pltpu.VMEM((tt, d), BF16), pltpu.VMEM((tt, d_ff), BF16)]
        + _conv_scratch(3, tt, 2 * d_ff, interleaved),
        compiler_params=_COMPILER_PARAMS, name="conv_gated_mlp",
    )(h, g, w_up, conv_w, w_down, carry_in, g_final)
    return res if emit_carry else (res[0], None)


def _trunk(h, params, carries, *, tt, interleaved, emit_carry):
    p = params
    mode = dict(tt=tt, interleaved=interleaved, emit_carry=emit_carry)
    new = {}
    h, new["sc"] = _short_conv_layer(h, p["g_mix0"], p["sc_w_in"], p["sc_conv_w"], p["sc_w_out"],
                                     carries["sc"], **mode)
    h, new["ffn0"] = _ffn_layer(h, p["g_ffn0"], p["ffn_w_up0"], p["ffn_conv_w0"], p["ffn_w_down0"],
                                carries["ffn0"], p["g_final"], final_norm=False, **mode)
    h, new["rg_conv"], new["rg_state"] = _rglru_layer(
        h, p["g_mix1"], p["rg_w_in"], p["rg_conv_w"], p["rg_conv_b"], p["rg_w_gate"],
        p["rg_b_gate_a"], p["rg_b_gate_x"], p["rg_lambda"], p["rg_w_out"],
        carries["rg_conv"], carries["rg_state"], **mode)
    h, new["ffn1"] = _ffn_layer(h, p["g_ffn1"], p["ffn_w_up1"], p["ffn_conv_w1"], p["ffn_w_down1"],
                                carries["ffn1"], p["g_final"], final_norm=True, **mode)
    return h, new


def kernel(x, meta_tokens, norm_mix_g, norm_ffn_g, final_norm_g, sc_w_in, sc_conv_w, sc_w_out,
           rg_w_in, rg_conv_w, rg_conv_b, rg_w_gate_a, rg_b_gate_a, rg_w_gate_x, rg_b_gate_x,
           rg_lambda, rg_w_out, ffn_w_up, ffn_conv_w, ffn_w_down):
    bsz, seq, d = x.shape
    n_meta = meta_tokens.shape[0]
    d_rnn = rg_w_out.shape[1]
    d_ff = ffn_w_down.shape[1]
    assert norm_mix_g.shape[0] == 2 and sc_w_in.shape[0] == 1 and rg_w_in.shape[0] == 1
    assert seq % TIME_TILE == 0 and n_meta % (2 * SUBLANES) == 0

    row = lambda v: v.reshape(1, -1).astype(F32)
    params = {
        "g_mix0": row(norm_mix_g[0]), "g_mix1": row(norm_mix_g[1]),
        "g_ffn0": row(norm_ffn_g[0]), "g_ffn1": row(norm_ffn_g[1]), "g_final": row(final_norm_g),
        "sc_w_in": sc_w_in[0].astype(BF16), "sc_conv_w": sc_conv_w[0], "sc_w_out": sc_w_out[0].astype(BF16),
        "rg_w_in": rg_w_in[0].astype(BF16), "rg_conv_w": rg_conv_w[0], "rg_conv_b": row(rg_conv_b[0]),
        "rg_w_gate": jnp.concatenate([rg_w_gate_a[0], rg_w_gate_x[0]], axis=-1).astype(BF16),
        "rg_b_gate_a": row(rg_b_gate_a[0]), "rg_b_gate_x": row(rg_b_gate_x[0]),
        "rg_lambda": row(rg_lambda[0]), "rg_w_out": rg_w_out[0].astype(BF16),
        "ffn_w_up0": ffn_w_up[0].astype(BF16), "ffn_conv_w0": ffn_conv_w[0], "ffn_w_down0": ffn_w_down[0].astype(BF16),
        "ffn_w_up1": ffn_w_up[1].astype(BF16), "ffn_conv_w1": ffn_conv_w[1], "ffn_w_down1": ffn_w_down[1].astype(BF16),
    }
    zero_carries = {
        "sc": jnp.zeros((SUBLANES, d), F32),
        "ffn0": jnp.zeros((SUBLANES, 2 * d_ff), F32),
        "rg_conv": jnp.zeros((SUBLANES, d_rnn), F32),
        "rg_state": jnp.zeros((1, d_rnn), F32),
        "ffn1": jnp.zeros((SUBLANES, 2 * d_ff), F32),
    }
    _, meta_carries = _trunk(meta_tokens.astype(x.dtype)[None], params, zero_carries,
                             tt=n_meta, interleaved=False, emit_carry=True)
    meta_carries = {k: v[0] for k, v in meta_carries.items()}
    out, _ = _trunk(x, params, meta_carries, tt=TIME_TILE, interleaved=True, emit_carry=False)
    return out
```
